```python
import math
import jax, jax.numpy as jnp
from jax import lax
import numpy as np

D_MODEL = 1024
BATCH = 2
SEQ = 8192
DEPTH = 2
DEC_BATCH = 128
DEC_SEQ = 1
PAST_LEN = 2048
PAGE_SIZE = 128

N_A_LAYERS = DEPTH // 2
N_B_LAYERS = DEPTH - N_A_LAYERS
POOL_WINDOWS = (2, 4, 8, 16)
N_POOL_GROUPS = len(POOL_WINDOWS)
POOL_GROUP_DIM = D_MODEL // N_POOL_GROUPS
POOL_HIST = max(POOL_WINDOWS) - 1
HEAD_DIM = 64
N_HEADS = D_MODEL // (2 * HEAD_DIM)
V_DIM = 2 * HEAD_DIM
ATTN_DIM = N_HEADS * 2 * HEAD_DIM
Q_BLOCK = 128
N_BUCKETS = 32
MAX_EXACT = N_BUCKETS // 2
MAX_DISTANCE = 128
N_GROUPS = 4
EXPERTS_PER_GROUP = 4
TOP_K_IN_GROUP = 2
D_EXPERT = D_MODEL // 2
EPS = 1e-6
MASK_VALUE = -1e30

kernel_name = "yoco_pool_diffattn_hmoe_step"


def rmsnorm(x, g):
    xf = x.astype(jnp.float32)
    y = xf * lax.rsqrt(jnp.mean(xf * xf, axis=-1, keepdims=True) + EPS)
    return (y * g.astype(jnp.float32)).astype(x.dtype)


def lambda_init(layer):
    return 0.8 - 0.6 * math.exp(-0.3 * layer)


def pool_mixer(u, hist, start_pos, w_pool, scale):
    B, T, _ = u.shape
    ext = u if hist is None else jnp.concatenate([hist.astype(u.dtype), u], axis=1)
    P = ext.shape[1] - T
    cs = jnp.cumsum(ext.astype(jnp.float32), axis=1)
    cs = jnp.concatenate([jnp.zeros((B, 1, D_MODEL), jnp.float32), cs], axis=1)
    hi = P + jnp.arange(T) + 1
    pos = start_pos + jnp.arange(T)
    diffs = []
    for g, w in enumerate(POOL_WINDOWS):
        sl = slice(g * POOL_GROUP_DIM, (g + 1) * POOL_GROUP_DIM)
        csg = cs[..., sl]
        lo = jnp.maximum(hi - w, 0)
        cnt = jnp.minimum(pos + 1, w).astype(jnp.float32)
        mean = (csg[:, hi] - csg[:, lo]) / cnt[None, :, None]
        diffs.append(mean - u[..., sl].astype(jnp.float32))
    d = jnp.stack(diffs, axis=2).astype(u.dtype)
    y = jnp.einsum('btgc,gcd->btgd', d, w_pool).reshape(B, T, D_MODEL)
    return y * scale, ext[:, -POOL_HIST:]


def hier_moe(x, w_rg, b_rg, w_re, b_re, w_gate, w_up, w_down):
    g_logits = (x @ w_rg).astype(jnp.float32) + b_rg.astype(jnp.float32)
    g_prob = jax.nn.softmax(g_logits, axis=-1)
    _, g_idx = lax.top_k(g_logits, 1)
    g_w = jnp.take_along_axis(g_prob, g_idx, axis=-1)[:, 0]
    e_logits = jnp.einsum('nd,gde->nge', x, w_re).astype(jnp.float32) + b_re.astype(jnp.float32)
    e_logits = jnp.take_along_axis(e_logits, g_idx[:, :, None], axis=1)[:, 0]
    top_v, top_i = lax.top_k(e_logits, TOP_K_IN_GROUP)
    top_w = jax.nn.softmax(top_v, axis=-1) * g_w[:, None]
    e_w = jnp.sum(jax.nn.one_hot(top_i, EXPERTS_PER_GROUP, dtype=jnp.float32) * top_w[..., None], axis=1)
    comb = jax.nn.one_hot(g_idx[:, 0], N_GROUPS, dtype=jnp.float32)[:, :, None] * e_w[:, None, :]
    y = jnp.zeros_like(x)
    for g in range(N_GROUPS):
        hg = jax.nn.silu(jnp.einsum('nd,edf->nef', x, w_gate[g])) * jnp.einsum('nd,edf->nef', x, w_up[g])
        y = y + jnp.einsum('nef,efd->nd', hg * comb[:, g, :, None].astype(hg.dtype), w_down[g])
    return y


def rel_bias_logits(q_pos, k_pos, rel_bias):
    dist = q_pos[:, None] - k_pos[None, :]
    n = jnp.maximum(dist, 0)
    nf = jnp.maximum(n, 1).astype(jnp.float32)
    large = MAX_EXACT + (jnp.log(nf / MAX_EXACT) / math.log(MAX_DISTANCE / MAX_EXACT)
                         * (N_BUCKETS - MAX_EXACT)).astype(jnp.int32)
    bucket = jnp.where(n < MAX_EXACT, n, jnp.minimum(large, N_BUCKETS - 1))
    bias = rel_bias.astype(jnp.float32)[bucket]
    bias = jnp.where((dist >= 0)[..., None], bias, MASK_VALUE)
    return jnp.transpose(bias, (2, 0, 1))


def diff_scores(q, k, bias):
    s = jnp.einsum('bqhcd,bkhcd->bchqk', q, k).astype(jnp.float32) * (HEAD_DIM ** -0.5)
    return s + bias[None, None]


def diff_weights(s, lam):
    p = jax.nn.softmax(s, axis=-1)
    return p[:, 0] - lam * p[:, 1]


def prompt_diff_attn(q, k, v, rel_bias, lam):
    B, T = q.shape[:2]
    nb = T // Q_BLOCK
    qb = q.reshape(B, nb, Q_BLOCK, N_HEADS, 2, HEAD_DIM).transpose(1, 0, 2, 3, 4, 5)
    k_pos = jnp.arange(T)
    vf = v.astype(jnp.float32)

    def block(args):
        qi, i = args
        q_pos = i * Q_BLOCK + jnp.arange(Q_BLOCK)
        a = diff_weights(diff_scores(qi, k, rel_bias_logits(q_pos, k_pos, rel_bias)), lam)
        return jnp.einsum('bhqk,bkhv->bqhv', a, vf)

    o = lax.map(block, (qb, jnp.arange(nb)))
    return o.transpose(1, 0, 2, 3, 4).reshape(B, T, N_HEADS, V_DIM)


def sample_diff_attn(q, k_new, v_new, cache_k, cache_v, page_table, rel_bias, lam):
    Bd, Tn = q.shape[:2]
    k_past = cache_k[page_table].reshape(Bd, PAST_LEN, N_HEADS, 2, HEAD_DIM).astype(q.dtype)
    v_past = cache_v[page_table].reshape(Bd, PAST_LEN, N_HEADS, V_DIM).astype(jnp.float32)
    q_pos = PAST_LEN + jnp.arange(Tn)
    s = jnp.concatenate([
        diff_scores(q, k_past, rel_bias_logits(q_pos, jnp.arange(PAST_LEN), rel_bias)),
        diff_scores(q, k_new, rel_bias_logits(q_pos, q_pos, rel_bias))], axis=-1)
    a = diff_weights(s, lam)
    o = (jnp.einsum('bhqk,bkhv->bqhv', a[..., :PAST_LEN], v_past)
         + jnp.einsum('bhqk,bkhv->bqhv', a[..., PAST_LEN:], v_new.astype(jnp.float32)))
    return o


def diff_out(o, g_sub, w_o, lam_init, dtype):
    B, T = o.shape[:2]
    o = o * lax.rsqrt(jnp.mean(o * o, axis=-1, keepdims=True) + EPS) * g_sub.astype(jnp.float32) * (1.0 - lam_init)
    return o.reshape(B, T, ATTN_DIM).astype(dtype) @ w_o


def shared_kv(h, norm_kv, w_kv):
    B, T = h.shape[:2]
    kv = rmsnorm(h, norm_kv) @ w_kv
    k = kv[..., :ATTN_DIM].reshape(B, T, N_HEADS, 2 * HEAD_DIM)
    v = kv[..., ATTN_DIM:].reshape(B, T, N_HEADS, V_DIM)
    return k, v


def setup_inputs(seed: int = 0) -> dict:
    key = jax.random.key(seed)
    ks = jax.random.split(key, 32)
    n_pages = PAST_LEN // PAGE_SIZE
    n_phys = (DEC_BATCH * n_pages * 5) // 4
    f32 = jnp.float32
    nrm = lambda k, shape, s: jax.random.normal(k, shape, f32) * s
    page_table = jax.random.permutation(ks[5], n_phys)[:DEC_BATCH * n_pages].reshape(DEC_BATCH, n_pages).astype(jnp.int32)
    return {
        "x_prompt": nrm(ks[0], (BATCH, SEQ, D_MODEL), 1.0),
        "x_sample": nrm(ks[1], (DEC_BATCH, DEC_SEQ, D_MODEL), 1.0),
        "state_pool": nrm(ks[2], (N_A_LAYERS, DEC_BATCH, POOL_HIST, D_MODEL), 1.0),
        "cache_k": nrm(ks[3], (n_phys, PAGE_SIZE, N_HEADS, 2 * HEAD_DIM), 1.0),
        "cache_v": nrm(ks[4], (n_phys, PAGE_SIZE, N_HEADS, V_DIM), 1.0),
        "page_table": page_table,
        "norm_mix": 1.0 + nrm(ks[6], (DEPTH, D_MODEL), 0.1),
        "norm_ffn": 1.0 + nrm(ks[7], (DEPTH, D_MODEL), 0.1),
        "norm_kv": 1.0 + nrm(ks[8], (D_MODEL,), 0.1),
        "norm_final": 1.0 + nrm(ks[9], (D_MODEL,), 0.1),
        "w_pool": nrm(ks[10], (N_A_LAYERS, N_POOL_GROUPS, POOL_GROUP_DIM, POOL_GROUP_DIM), POOL_GROUP_DIM ** -0.5),
        "pool_scale": 1.0 + nrm(ks[11], (N_A_LAYERS, D_MODEL), 0.1),
        "w_q": nrm(ks[12], (N_B_LAYERS, D_MODEL, ATTN_DIM), D_MODEL ** -0.5),
        "w_kv": nrm(ks[13], (D_MODEL, 2 * ATTN_DIM), D_MODEL ** -0.5),
        "w_o": nrm(ks[14], (N_B_LAYERS, ATTN_DIM, D_MODEL), ATTN_DIM ** -0.5),
        "g_sub": 1.0 + nrm(ks[15], (N_B_LAYERS, V_DIM), 0.1),
        "lam_q1": nrm(ks[16], (N_B_LAYERS, HEAD_DIM), 0.1),
        "lam_k1": nrm(ks[17], (N_B_LAYERS, HEAD_DIM), 0.1),
        "lam_q2": nrm(ks[18], (N_B_LAYERS, HEAD_DIM), 0.1),
        "lam_k2": nrm(ks[19], (N_B_LAYERS, HEAD_DIM), 0.1),
        "rel_bias": nrm(ks[20], (N_BUCKETS, N_HEADS), 0.5),
        "w_router_group": nrm(ks[21], (DEPTH, D_MODEL, N_GROUPS), D_MODEL ** -0.5),
        "b_router_group": nrm(ks[22], (DEPTH, N_GROUPS), 0.01),
        "w_router_expert": nrm(ks[23], (DEPTH, N_GROUPS, D_MODEL, EXPERTS_PER_GROUP), D_MODEL ** -0.5),
        "b_router_expert": nrm(ks[24], (DEPTH, N_GROUPS, EXPERTS_PER_GROUP), 0.01),
        "w_gate": nrm(ks[25], (DEPTH, N_GROUPS, EXPERTS_PER_GROUP, D_MODEL, D_EXPERT), D_MODEL ** -0.5),
        "w_up": nrm(ks[26], (DEPTH, N_GROUPS, EXPERTS_PER_GROUP, D_MODEL, D_EXPERT), D_MODEL ** -0.5),
        "w_down": nrm(ks[27], (DEPTH, N_GROUPS, EXPERTS_PER_GROUP, D_EXPERT, D_MODEL), D_EXPERT ** -0.5),
    }


def reference(x_prompt, x_sample, state_pool, cache_k, cache_v, page_table,
              norm_mix, norm_ffn, norm_kv, norm_final, w_pool, pool_scale,
              w_q, w_kv, w_o, g_sub, lam_q1, lam_k1, lam_q2, lam_k2, rel_bias,
              w_router_group, b_router_group, w_router_expert, b_router_expert,
              w_gate, w_up, w_down):
    hp, hs = x_prompt, x_sample
    Bp, Tp = hp.shape[:2]
    Bs, Ts = hs.shape[:2]
    pool_p, pool_s = [], []
    k_p = v_p = k_s = v_s = None
    for layer in range(DEPTH):
        up = rmsnorm(hp, norm_mix[layer])
        us = rmsnorm(hs, norm_mix[layer])
        if layer < N_A_LAYERS:
            a = layer
            yp, hist_p = pool_mixer(up, None, 0, w_pool[a], pool_scale[a])
            ys, hist_s = pool_mixer(us, state_pool[a], PAST_LEN, w_pool[a], pool_scale[a])
            pool_p.append(hist_p)
            pool_s.append(hist_s)
        else:
            b = layer - N_A_LAYERS
            lam_init = lambda_init(layer)
            lam = (jnp.exp(jnp.sum(lam_q1[b].astype(jnp.float32) * lam_k1[b].astype(jnp.float32)))
                   - jnp.exp(jnp.sum(lam_q2[b].astype(jnp.float32) * lam_k2[b].astype(jnp.float32))) + lam_init)
            qp = (up @ w_q[b]).reshape(Bp, Tp, N_HEADS, 2, HEAD_DIM)
            qs = (us @ w_q[b]).reshape(Bs, Ts, N_HEADS, 2, HEAD_DIM)
            op = prompt_diff_attn(qp, k_p.reshape(Bp, Tp, N_HEADS, 2, HEAD_DIM), v_p, rel_bias, lam)
            os_ = sample_diff_attn(qs, k_s.reshape(Bs, Ts, N_HEADS, 2, HEAD_DIM), v_s,
                                   cache_k, cache_v, page_table, rel_bias, lam)
            yp = diff_out(op, g_sub[b], w_o[b], lam_init, hp.dtype)
            ys = diff_out(os_, g_sub[b], w_o[b], lam_init, hs.dtype)
        hp = hp + yp
        hs = hs + ys
        moe_args = (w_router_group[layer], b_router_group[layer], w_router_expert[layer],
                    b_router_expert[layer], w_gate[layer], w_up[layer], w_down[layer])
        hp = hp + hier_moe(rmsnorm(hp, norm_ffn[layer]).reshape(Bp * Tp, D_MODEL), *moe_args).reshape(Bp, Tp, D_MODEL)
        hs = hs + hier_moe(rmsnorm(hs, norm_ffn[layer]).reshape(Bs * Ts, D_MODEL), *moe_args).reshape(Bs, Ts, D_MODEL)
        if layer == N_A_LAYERS - 1:
            k_p, v_p = shared_kv(hp, norm_kv, w_kv)
            k_s, v_s = shared_kv(hs, norm_kv, w_kv)
    y_prompt = rmsnorm(hp, norm_final)
    y_sample = rmsnorm(hs, norm_final)
    new_pool_prompt = jnp.stack(pool_p, axis=0)
    new_pool_sample = jnp.stack(pool_s, axis=0)
    return (y_prompt, y_sample, new_pool_prompt, new_pool_sample, k_p, v_p, k_s, v_s)
```

```python
import functools
import math

import numpy as np
import jax
import jax.numpy as jnp
from jax import lax
from jax.experimental import pallas as pl
from jax.experimental.pallas import tpu as pltpu

D_MODEL = 1024
PAST_LEN = 2048
PAGE_SIZE = 128
N_PAGES = PAST_LEN // PAGE_SIZE
POOL_WINDOWS = (2, 4, 8, 16)
POOL_GROUP_DIM = D_MODEL // len(POOL_WINDOWS)
POOL_HIST = max(POOL_WINDOWS) - 1
POOL_CARRY = 16
HEAD_DIM = 64
N_HEADS = 8
V_DIM = 2 * HEAD_DIM
N_BUCKETS = 32
MAX_EXACT = N_BUCKETS // 2
MAX_DISTANCE = 128
N_GROUPS = 4
EXPERTS_PER_GROUP = 4
N_EXPERTS = N_GROUPS * EXPERTS_PER_GROUP
D_EXPERT = D_MODEL // 2
EPS = 1e-6
MASK_VALUE = -1e30
LANES = 128
ROUTER_EXPERT_LANE0 = N_GROUPS
VMEM_LIMIT = 56 * 1024 * 1024

F32 = jnp.float32
BF16 = jnp.bfloat16
HIGHEST = lax.Precision.HIGHEST


def _lambda_init(layer):
    return 0.8 - 0.6 * math.exp(-0.3 * layer)


def _rmsnorm(x, g):
    y = x * lax.rsqrt(jnp.mean(x * x, axis=-1, keepdims=True) + EPS)
    return y * g


def _params(*sem):
    return pltpu.CompilerParams(dimension_semantics=sem, vmem_limit_bytes=VMEM_LIMIT)


def _pool_prompt_kernel(x_ref, g_ref, w_ref, sc_ref, h_ref, hist_ref, ext_ref, *, tb, nt):
    t = pl.program_id(1)
    x = x_ref[0]
    u = _rmsnorm(x, g_ref[...])

    @pl.when(t == 0)
    def _():
        ext_ref[0:POOL_CARRY, :] = jnp.zeros((POOL_CARRY, D_MODEL), F32)

    ext_ref[POOL_CARRY:POOL_CARRY + tb, :] = u
    pos = t * tb + lax.broadcasted_iota(jnp.int32, (tb, 1), 0)
    for g, w in enumerate(POOL_WINDOWS):
        sl = slice(g * POOL_GROUP_DIM, (g + 1) * POOL_GROUP_DIM)
        cur = ext_ref[:, sl]
        sh = 1
        while sh < w:
            cur = cur + pltpu.roll(cur, sh, axis=0)
            sh *= 2
        cnt = jnp.minimum(pos + 1, w).astype(F32)
        d = cur[POOL_CARRY:, :] / cnt - u[:, sl]
        y = jnp.dot(d, w_ref[g], precision=HIGHEST, preferred_element_type=F32)
        h_ref[0, :, sl] = x[:, sl] + y * sc_ref[:, sl]

    ext_ref[0:POOL_CARRY, :] = u[tb - POOL_CARRY:, :]

    @pl.when(t == nt - 1)
    def _():
        hist_ref[0] = u[tb - POOL_HIST:, :]


def _pool_prompt(x, g, w_pool, scale, tb=512):
    b, t, d = x.shape
    nt = t // tb
    return pl.pallas_call(
        functools.partial(_pool_prompt_kernel, tb=tb, nt=nt),
        grid=(b, nt),
        in_specs=[
            pl.BlockSpec((1, tb, d), lambda i, j: (i, j, 0)),
            pl.BlockSpec((1, d), lambda i, j: (0, 0)),
            pl.BlockSpec(w_pool.shape, lambda i, j: (0, 0, 0)),
            pl.BlockSpec((1, d), lambda i, j: (0, 0)),
        ],
        out_specs=[
            pl.BlockSpec((1, tb, d), lambda i, j: (i, j, 0)),
            pl.BlockSpec((1, POOL_HIST, d), lambda i, j: (i, 0, 0)),
        ],
        out_shape=[
            jax.ShapeDtypeStruct((b, t, d), F32),
            jax.ShapeDtypeStruct((b, POOL_HIST, d), F32),
        ],
        scratch_shapes=[pltpu.VMEM((POOL_CARRY + tb, d), F32)],
        compiler_params=_params("arbitrary", "arbitrary"),
        name="pool_prompt",
    )(x, g, w_pool, scale)


def _pool_sample_kernel(x_ref, st_ref, g_ref, w_ref, sc_ref, h_ref, new_ref):
    x = x_ref[...]
    u = _rmsnorm(x, g_ref[...])
    for g, w in enumerate(POOL_WINDOWS):
        sl = slice(g * POOL_GROUP_DIM, (g + 1) * POOL_GROUP_DIM)
        s = u[:, sl]
        for j in range(1, w):
            s = s + st_ref[:, POOL_HIST - j, sl]
        d = s / float(w) - u[:, sl]
        y = jnp.dot(d, w_ref[g], precision=HIGHEST, preferred_element_type=F32)
        h_ref[:, sl] = x[:, sl] + y * sc_ref[:, sl]
    for j in range(POOL_HIST - 1):
        new_ref[:, j, :] = st_ref[:, j + 1, :]
    new_ref[:, POOL_HIST - 1, :] = u


def _pool_sample(x, state, g, w_pool, scale):
    n, d = x.shape
    assert PAST_LEN + 1 >= max(POOL_WINDOWS)
    return pl.pallas_call(
        _pool_sample_kernel,
        out_shape=[
            jax.ShapeDtypeStruct((n, d), F32),
            jax.ShapeDtypeStruct((n, POOL_HIST, d), F32),
        ],
        compiler_params=pltpu.CompilerParams(vmem_limit_bytes=VMEM_LIMIT),
        name="pool_sample",
    )(x, state, g, w_pool, scale)


def _route(logits):
    lane = lax.broadcasted_iota(jnp.int32, logits.shape, 1)
    lane_f = lane.astype(F32)
    big = float(LANES)
    neg = -jnp.inf
    gl = jnp.where(lane < N_GROUPS, logits, neg)
    gmax = jnp.max(gl, axis=-1, keepdims=True)
    gidx = jnp.min(jnp.where(gl == gmax, lane_f, big), axis=-1, keepdims=True)
    gsum = jnp.sum(jnp.where(lane < N_GROUPS, jnp.exp(logits - gmax), 0.0), axis=-1, keepdims=True)
    g_w = 1.0 / gsum
    lo = ROUTER_EXPERT_LANE0 + EXPERTS_PER_GROUP * gidx
    el = jnp.where((lane_f >= lo) & (lane_f < lo + EXPERTS_PER_GROUP), logits, neg)
    e1 = jnp.max(el, axis=-1, keepdims=True)
    i1 = jnp.min(jnp.where(el == e1, lane_f, big), axis=-1, keepdims=True)
    el2 = jnp.where(lane_f == i1, neg, el)
    e2 = jnp.max(el2, axis=-1, keepdims=True)
    i2 = jnp.min(jnp.where(el2 == e2, lane_f, big), axis=-1, keepdims=True)
    tt = jnp.exp(e2 - e1)
    den = 1.0 + tt
    w1 = g_w / den
    w2 = g_w * tt / den
    return jnp.where(lane_f == i1, w1, 0.0) + jnp.where(lane_f == i2, w2, 0.0)


def _moe_dense_kernel(h_ref, g_ref, wr_ref, br_ref, wg_ref, wu_ref, wd_ref, *rest, final_norm):
    if final_norm:
        gf_ref, out_ref, y_ref, xn_ref, comb_ref, acc_ref = rest
    else:
        out_ref, xn_ref, comb_ref, acc_ref = rest
    e = pl.program_id(1)

    @pl.when(e == 0)
    def _():
        xn = _rmsnorm(h_ref[...], g_ref[...])
        logits = jnp.dot(xn, wr_ref[...], precision=HIGHEST, preferred_element_type=F32) + br_ref[...]
        comb_ref[...] = _route(logits)
        xn_ref[...] = xn.astype(BF16)
        acc_ref[...] = jnp.zeros_like(acc_ref)

    comb = comb_ref[...]
    lane = lax.broadcasted_iota(jnp.int32, comb.shape, 1)
    ce = jnp.sum(jnp.where(lane == ROUTER_EXPERT_LANE0 + e, comb, 0.0), axis=-1, keepdims=True)
    xn = xn_ref[...]
    gate = jnp.dot(xn, wg_ref[0], preferred_element_type=F32)
    up = jnp.dot(xn, wu_ref[0], preferred_element_type=F32)
    hmid = gate * (1.0 / (1.0 + jnp.exp(-gate))) * up * ce
    acc_ref[...] += jnp.dot(hmid.astype(BF16), wd_ref[0], preferred_element_type=F32)

    @pl.when(e == N_EXPERTS - 1)
    def _():
        hn = h_ref[...] + acc_ref[...]
        out_ref[...] = hn
        if final_norm:
            y_ref[...] = _rmsnorm(hn, gf_ref[...])


def _moe(h, g_ffn, w_router, b_router, wg, wu, wd, g_final=None, tm=1024):
    n, d = h.shape
    tm = min(tm, n)
    final_norm = g_final is not None
    row = lambda i, e: (i, 0)
    const2 = lambda i, e: (0, 0)
    in_specs = [
        pl.BlockSpec((tm, d), row),
        pl.BlockSpec((1, d), const2),
        pl.BlockSpec((d, LANES), const2),
        pl.BlockSpec((1, LANES), const2),
        pl.BlockSpec((1, d, D_EXPERT), lambda i, e: (e, 0, 0)),
        pl.BlockSpec((1, d, D_EXPERT), lambda i, e: (e, 0, 0)),
        pl.BlockSpec((1, D_EXPERT, d), lambda i, e: (e, 0, 0)),
    ]
    args = [h, g_ffn, w_router, b_router, wg, wu, wd]
    out_specs = [pl.BlockSpec((tm, d), row)]
    out_shape = [jax.ShapeDtypeStruct((n, d), F32)]
    if final_norm:
        in_specs.append(pl.BlockSpec((1, d), const2))
        args.append(g_final)
        out_specs.append(pl.BlockSpec((tm, d), row))
        out_shape.append(jax.ShapeDtypeStruct((n, d), F32))
    res = pl.pallas_call(
        functools.partial(_moe_dense_kernel, final_norm=final_norm),
        grid=(n // tm, N_EXPERTS),
        in_specs=in_specs,
        out_specs=out_specs,
        out_shape=out_shape,
        scratch_shapes=[
            pltpu.VMEM((tm, d), BF16),
            pltpu.VMEM((tm, LANES), F32),
            pltpu.VMEM((tm, d), F32),
        ],
        compiler_params=_params("parallel", "arbitrary"),
        name="moe_final" if final_norm else "moe",
    )(*args)
    return res if final_norm else res[0]


def _qkv_kernel(h_ref, gkv_ref, gq_ref, wkv_ref, wq_ref, *outs, head_major):
    h = h_ref[0]
    xkv = _rmsnorm(h, gkv_ref[...]).astype(BF16)
    xq = _rmsnorm(h, gq_ref[...]).astype(BF16)
    kv = jnp.dot(xkv, wkv_ref[...], preferred_element_type=F32)
    q = jnp.dot(xq, wq_ref[...], preferred_element_type=F32) * (HEAD_DIM ** -0.5)
    k = kv[:, :D_MODEL]
    v = kv[:, D_MODEL:]
    if head_major:
        k_ref, v_ref, kb_ref, vb_ref, qz_ref = outs
        k_ref[0] = k
        v_ref[0] = v
        lane = lax.broadcasted_iota(jnp.int32, (h.shape[0], V_DIM), 1)
        for hd in range(N_HEADS):
            sl = slice(hd * V_DIM, (hd + 1) * V_DIM)
            kb_ref[0, hd] = k[:, sl].astype(BF16)
            vb_ref[0, hd] = v[:, sl].astype(BF16)
            qh = q[:, sl]
            qz_ref[0, hd, 0] = jnp.where(lane < HEAD_DIM, qh, 0.0).astype(BF16)
            qz_ref[0, hd, 1] = jnp.where(lane >= HEAD_DIM, qh, 0.0).astype(BF16)
    else:
        k_ref, v_ref, q_ref = outs
        k_ref[0] = k
        v_ref[0] = v
        q_ref[0] = q


def _qkv(h, g_kv, g_q, w_kv, w_q, head_major, tm=512):
    b, t, d = h.shape
    tm = min(tm, t)
    row = lambda i, j: (i, j, 0)
    const2 = lambda i, j: (0, 0)
    out_specs = [pl.BlockSpec((1, tm, d), row), pl.BlockSpec((1, tm, d), row)]
    out_shape = [jax.ShapeDtypeStruct((b, t, d), F32), jax.ShapeDtypeStruct((b, t, d), F32)]
    if head_major:
        hm = lambda i, j: (i, 0, j, 0)
        out_specs += [
            pl.BlockSpec((1, N_HEADS, tm, V_DIM), hm),
            pl.BlockSpec((1, N_HEADS, tm, V_DIM), hm),
            pl.BlockSpec((1, N_HEADS, 2, tm, V_DIM), lambda i, j: (i, 0, 0, j, 0)),
        ]
        out_shape += [
            jax.ShapeDtypeStruct((b, N_HEADS, t, V_DIM), BF16),
            jax.ShapeDtypeStruct((b, N_HEADS, t, V_DIM), BF16),
            jax.ShapeDtypeStruct((b, N_HEADS, 2, t, V_DIM), BF16),
        ]
    else:
        out_specs.append(pl.BlockSpec((1, tm, d), row))
        out_shape.append(jax.ShapeDtypeStruct((b, t, d), F32))
    return pl.pallas_call(
        functools.partial(_qkv_kernel, head_major=head_major),
        grid=(b, t // tm),
        in_specs=[
            pl.BlockSpec((1, tm, d), row),
            pl.BlockSpec((1, d), const2),
            pl.BlockSpec((1, d), const2),
            pl.BlockSpec((d, 2 * d), const2),
            pl.BlockSpec((d, d), const2),
        ],
        out_specs=out_specs,
        out_shape=out_shape,
        compiler_params=_params("parallel", "parallel"),
        name="qkv_prompt" if head_major else "qkv_sample",
    )(h, g_kv, g_q, w_kv, w_q)


def _bucket_np(dist):
    n = np.maximum(dist, 0)
    nf = np.maximum(n, 1).astype(np.float32)
    large = MAX_EXACT + (np.log(nf / np.float32(MAX_EXACT)) / np.float32(math.log(MAX_DISTANCE / MAX_EXACT))
                         * np.float32(N_BUCKETS - MAX_EXACT)).astype(np.int32)
    bucket = np.where(n < MAX_EXACT, n, np.minimum(large, N_BUCKETS - 1))
    return np.where(dist >= 0, bucket, -1).astype(np.int32)


def _bias_table_kernel(rb_ref, bucket_ref, out_ref):
    hd = pl.program_id(0)
    bucket = bucket_ref[...]
    far = rb_ref[hd, N_BUCKETS - 1]
    acc = jnp.full(bucket.shape, MASK_VALUE, F32)
    for bk in range(N_BUCKETS):
        acc = jnp.where(bucket == bk, rb_ref[hd, bk] - far, acc)
    out_ref[0] = acc


def _bias_table(rel_bias_t, bucket):
    shape = bucket.shape
    nd = len(shape)
    return pl.pallas_call(
        _bias_table_kernel,
        grid=(N_HEADS,),
        in_specs=[
            pl.BlockSpec(memory_space=pltpu.SMEM),
            pl.BlockSpec(shape, lambda i: (0,) * nd),
        ],
        out_specs=pl.BlockSpec((1,) + shape, lambda i: (i,) + (0,) * nd),
        out_shape=jax.ShapeDtypeStruct((N_HEADS,) + shape, F32),
        compiler_params=_params("parallel"),
        name="bias_table",
    )(rel_bias_t, bucket)


def _lambda_value(lamv_ref, lam_init):
    lv = lamv_ref[...]
    s1 = jnp.sum(lv[0:1] * lv[1:2], axis=-1, keepdims=True)
    s2 = jnp.sum(lv[2:3] * lv[3:4], axis=-1, keepdims=True)
    return jnp.exp(s1) - jnp.exp(s2) + lam_init


def _head_out(o, gsub, lam_init):
    o = o * lax.rsqrt(jnp.mean(o * o, axis=-1, keepdims=True) + EPS)
    return o * gsub * (1.0 - lam_init)


def _attn_prompt_kernel(q_ref, k_ref, v_ref, tab_ref, lamv_ref, gsub_ref, o_ref,
                        m_ref, l_ref, acc_ref, *, tq, ck, lam_init):
    qi = pl.program_id(2)
    q = q_ref[0, 0].reshape(2 * tq, V_DIM)

    def scores(k0, size):
        k = k_ref[0, 0, pl.ds(k0, size), :]
        return lax.dot_general(q, k, (((1,), (1,)), ((), ())), preferred_element_type=F32)

    def update(s, k0, size, first):
        v = v_ref[0, 0, pl.ds(k0, size), :]
        smax = jnp.max(s, axis=-1, keepdims=True)
        if first:
            m_new = smax
            p = jnp.exp(s - m_new)
            l_ref[...] = jnp.sum(p, axis=-1, keepdims=True)
            acc_ref[...] = jnp.dot(p.astype(BF16), v, preferred_element_type=F32)
        else:
            m_old = m_ref[...]
            m_new = jnp.maximum(m_old, smax)
            alpha = jnp.exp(m_old - m_new)
            p = jnp.exp(s - m_new)
            l_ref[...] = alpha * l_ref[...] + jnp.sum(p, axis=-1, keepdims=True)
            acc_ref[...] = alpha * acc_ref[...] + jnp.dot(p.astype(BF16), v, preferred_element_type=F32)
        m_ref[...] = m_new

    def near(k0, which, first):
        s = scores(k0, tq).reshape(2, tq, tq) + tab_ref[0, which][None]
        update(s.reshape(2 * tq, tq), k0, tq, first)

    near(pl.multiple_of(qi * tq, tq), 0, True)

    @pl.when(qi >= 1)
    def _():
        near(pl.multiple_of((qi - 1) * tq, tq), 1, False)

    n_far = jnp.maximum(qi - 1, 0) * tq
    per = ck // tq
    n_big = jnp.maximum(qi - 1, 0) // per

    def far_body(c, carry):
        k0 = pl.multiple_of(c * ck, ck)
        update(scores(k0, ck), k0, ck, False)
        return carry

    lax.fori_loop(0, n_big, far_body, 0)
    for r in range(per - 1):
        @pl.when(n_far - n_big * ck > r * tq)
        def _():
            k0 = pl.multiple_of(n_big * ck + r * tq, tq)
            update(scores(k0, tq), k0, tq, False)

    lam = _lambda_value(lamv_ref, lam_init)
    on = acc_ref[...] / l_ref[...]
    o = on[:tq] - lam * on[tq:]
    o_ref[0] = _head_out(o, gsub_ref[...], lam_init).astype(o_ref.dtype)


def _attn_prompt(qz, kb, vb, tab, lamv, gsub, lam_init, tq, ck=512):
    b, nh, _, t, _ = qz.shape
    return pl.pallas_call(
        functools.partial(_attn_prompt_kernel, tq=tq, ck=ck, lam_init=lam_init),
        grid=(b, nh, t // tq),
        in_specs=[
            pl.BlockSpec((1, 1, 2, tq, V_DIM), lambda i, h, j: (i, h, 0, j, 0)),
            pl.BlockSpec((1, 1, t, V_DIM), lambda i, h, j: (i, h, 0, 0)),
            pl.BlockSpec((1, 1, t, V_DIM), lambda i, h, j: (i, h, 0, 0)),
            pl.BlockSpec((1, 2, tq, tq), lambda i, h, j: (h, 0, 0, 0)),
            pl.BlockSpec(lamv.shape, lambda i, h, j: (0, 0)),
            pl.BlockSpec((1, V_DIM), lambda i, h, j: (0, 0)),
        ],
        out_specs=pl.BlockSpec((1, tq, V_DIM), lambda i, h, j: (i, j, h)),
        out_shape=jax.ShapeDtypeStruct((b, t, nh * V_DIM), BF16),
        scratch_shapes=[
            pltpu.VMEM((2 * tq, 1), F32),
            pltpu.VMEM((2 * tq, 1), F32),
            pltpu.VMEM((2 * tq, V_DIM), F32),
        ],
        compiler_params=_params("parallel", "parallel", "arbitrary"),
        name="attn_prompt",
    )(qz, kb, vb, tab, lamv, gsub)


def _attn_sample_kernel(pt_ref, q_ref, kn_ref, vn_ref, ck_ref, cv_ref, dlast_ref, dnew_ref,
                        lamv_ref, gsub_ref, o_ref, m_ref, l_ref, acc_ref, *, lam_init):
    p = pl.program_id(1)
    q = q_ref[0]
    lane = lax.broadcasted_iota(jnp.int32, q.shape, 1)
    qs = (jnp.where(lane < HEAD_DIM, q, 0.0), jnp.where(lane >= HEAD_DIM, q, 0.0))

    @pl.when(p == 0)
    def _():
        m_ref[...] = jnp.full(m_ref.shape, MASK_VALUE, F32)
        l_ref[...] = jnp.zeros_like(l_ref)
        acc_ref[...] = jnp.zeros_like(acc_ref)

    def step(kp, vp, delta):
        for c in range(2):
            s = jnp.sum(kp * qs[c][None], axis=-1, keepdims=True)
            if delta is not None:
                s = s + delta
            m_old = m_ref[c]
            m_new = jnp.maximum(m_old, jnp.max(s, axis=0))
            alpha = jnp.exp(m_old - m_new)
            pr = jnp.exp(s - m_new[None])
            l_ref[c] = alpha * l_ref[c] + jnp.sum(pr, axis=0)
            acc_ref[c] = alpha * acc_ref[c] + jnp.sum(pr * vp, axis=0)
            m_ref[c] = m_new

    @pl.when(p < N_PAGES - 1)
    def _():
        step(ck_ref[0], cv_ref[0], None)

    @pl.when(p == N_PAGES - 1)
    def _():
        step(ck_ref[0], cv_ref[0], dlast_ref[...])
        step(kn_ref[...], vn_ref[...], dnew_ref[...][None])
        lam = _lambda_value(lamv_ref, lam_init)
        o = acc_ref[0] / l_ref[0] - lam * (acc_ref[1] / l_ref[1])
        o_ref[0] = _head_out(o, gsub_ref[...], lam_init)


def _attn_sample(page_table, q, k_new, v_new, cache_k, cache_v, dlast, dnew, lamv, gsub, lam_init):
    n = q.shape[0]
    blk = (1, N_HEADS, V_DIM)
    seq = lambda i, p, pt: (i, 0, 0)
    page = lambda i, p, pt: (pt[i * N_PAGES + p], 0, 0, 0)
    grid_spec = pltpu.PrefetchScalarGridSpec(
        num_scalar_prefetch=1,
        grid=(n, N_PAGES),
        in_specs=[
            pl.BlockSpec(blk, seq),
            pl.BlockSpec(blk, seq),
            pl.BlockSpec(blk, seq),
            pl.BlockSpec((1, PAGE_SIZE, N_HEADS, V_DIM), page),
            pl.BlockSpec((1, PAGE_SIZE, N_HEADS, V_DIM), page),
            pl.BlockSpec(dlast.shape, lambda i, p, pt: (0, 0, 0)),
            pl.BlockSpec(dnew.shape, lambda i, p, pt: (0, 0)),
            pl.BlockSpec(lamv.shape, lambda i, p, pt: (0, 0)),
            pl.BlockSpec((1, V_DIM), lambda i, p, pt: (0, 0)),
        ],
        out_specs=pl.BlockSpec(blk, seq),
        scratch_shapes=[
            pltpu.VMEM((2, N_HEADS, 1), F32),
            pltpu.VMEM((2, N_HEADS, 1), F32),
            pltpu.VMEM((2, N_HEADS, V_DIM), F32),
        ],
    )
    return pl.pallas_call(
        functools.partial(_attn_sample_kernel, lam_init=lam_init),
        grid_spec=grid_spec,
        out_shape=jax.ShapeDtypeStruct((n, N_HEADS, V_DIM), F32),
        compiler_params=_params("parallel", "arbitrary"),
        name="attn_sample",
    )(page_table, q, k_new, v_new, cache_k, cache_v, dlast, dnew, lamv, gsub)


def _oproj_kernel(o_ref, w_ref, h_ref, out_ref):
    out_ref[...] = h_ref[...] + jnp.dot(o_ref[...].astype(BF16), w_ref[...], preferred_element_type=F32)


def _oproj(o, w_o, h, tm=1024):
    n, d = h.shape
    tm = min(tm, n)
    row = lambda i: (i, 0)
    return pl.pallas_call(
        _oproj_kernel,
        grid=(n // tm,),
        in_specs=[pl.BlockSpec((tm, d), row), pl.BlockSpec((d, d), lambda i: (0, 0)), pl.BlockSpec((tm, d), row)],
        out_specs=pl.BlockSpec((tm, d), row),
        out_shape=jax.ShapeDtypeStruct((n, d), F32),
        compiler_params=_params("parallel"),
        name="oproj",
    )(o, w_o, h)


def kernel(x_prompt, x_sample, state_pool, cache_k, cache_v, page_table, norm_mix, norm_ffn, norm_kv, norm_final, w_pool, pool_scale, w_q, w_kv, w_o, g_sub, lam_q1, lam_k1, lam_q2, lam_k2, rel_bias, w_router_group, b_router_group, w_router_expert, b_router_expert, w_gate, w_up, w_down):
    bp, tp, d = x_prompt.shape
    bs = x_sample.shape[0]
    n_p = bp * tp
    tq = 256

    def router_params(layer):
        wr = jnp.concatenate(
            [w_router_group[layer], jnp.transpose(w_router_expert[layer], (1, 0, 2)).reshape(d, N_EXPERTS)], axis=1)
        wr = jnp.pad(wr, ((0, 0), (0, LANES - wr.shape[1])))
        br = jnp.concatenate([b_router_group[layer], b_router_expert[layer].reshape(N_EXPERTS)])
        br = jnp.pad(br, (0, LANES - br.shape[0])).reshape(1, LANES)
        return wr, br

    def expert_params(layer):
        return (w_gate[layer].reshape(N_EXPERTS, d, D_EXPERT).astype(BF16),
                w_up[layer].reshape(N_EXPERTS, d, D_EXPERT).astype(BF16),
                w_down[layer].reshape(N_EXPERTS, D_EXPERT, d).astype(BF16))

    g_mix0 = norm_mix[0].reshape(1, d)
    sc0 = pool_scale[0].reshape(1, d)
    h1p, pool_p = _pool_prompt(x_prompt, g_mix0, w_pool[0], sc0)
    h1s, pool_s = _pool_sample(x_sample.reshape(bs, d), state_pool[0], g_mix0, w_pool[0], sc0)

    wr0, br0 = router_params(0)
    ex0 = expert_params(0)
    g_ffn0 = norm_ffn[0].reshape(1, d)
    h2p = _moe(h1p.reshape(n_p, d), g_ffn0, wr0, br0, *ex0)
    h2s = _moe(h1s, g_ffn0, wr0, br0, *ex0)

    g_kv = norm_kv.reshape(1, d)
    g_mix1 = norm_mix[1].reshape(1, d)
    w_kv_b = w_kv.astype(BF16)
    w_q_b = w_q[0].astype(BF16)
    k_p, v_p, kb, vb, qz = _qkv(h2p.reshape(bp, tp, d), g_kv, g_mix1, w_kv_b, w_q_b, True)
    k_s, v_s, q_s = _qkv(h2s.reshape(1, bs, d), g_kv, g_mix1, w_kv_b, w_q_b, False)

    lam_init = _lambda_init(1)
    lamv = jnp.concatenate([lam_q1, lam_k1, lam_q2, lam_k2], axis=0)
    gsub = g_sub[0].reshape(1, V_DIM)
    rel_t = jnp.transpose(rel_bias)

    ii = np.arange(tq)[:, None]
    jj = np.arange(tq)[None, :]
    bucket_p = np.stack([_bucket_np(ii - jj), _bucket_np(ii - jj + tq)], axis=0)
    tab_p = _bias_table(rel_t, jnp.asarray(bucket_p))
    o_p = _attn_prompt(qz, kb, vb, tab_p, lamv, gsub, lam_init, tq)

    kpos_last = PAST_LEN - PAGE_SIZE + np.arange(PAGE_SIZE)
    bucket_s = np.concatenate([_bucket_np(PAST_LEN - kpos_last), _bucket_np(np.zeros((1,), np.int64))])
    tab_s = _bias_table(rel_t, jnp.asarray(bucket_s.reshape(1, PAGE_SIZE + 1)))
    tab_s = jnp.transpose(tab_s[:, 0, :])
    dlast = tab_s[:PAGE_SIZE].reshape(PAGE_SIZE, N_HEADS, 1)
    dnew = tab_s[PAGE_SIZE].reshape(N_HEADS, 1)
    o_s = _attn_sample(page_table.reshape(-1), q_s.reshape(bs, N_HEADS, V_DIM),
                       k_s.reshape(bs, N_HEADS, V_DIM), v_s.reshape(bs, N_HEADS, V_DIM),
                       cache_k, cache_v, dlast, dnew, lamv, gsub, lam_init)

    w_o_b = w_o[0].astype(BF16)
    h3p = _oproj(o_p.reshape(n_p, d), w_o_b, h2p)
    h3s = _oproj(o_s.reshape(bs, d), w_o_b, h2s)

    wr1, br1 = router_params(1)
    ex1 = expert_params(1)
    g_ffn1 = norm_ffn[1].reshape(1, d)
    g_fin = norm_final.reshape(1, d)
    _, y_p = _moe(h3p, g_ffn1, wr1, br1, *ex1, g_final=g_fin)
    _, y_s = _moe(h3s, g_ffn1, wr1, br1, *ex1, g_final=g_fin)

    return (y_p.reshape(bp, tp, d), y_s.reshape(bs, 1, d),
            pool_p[None], pool_s[None],
            k_p.reshape(bp, tp, N_HEADS, V_DIM), v_p.reshape(bp, tp, N_HEADS, V_DIM),
            k_s.reshape(bs, 1, N_HEADS, V_DIM), v_s.reshape(bs, 1, N_HEADS, V_DIM))
```
